```python
import jax, jax.numpy as jnp
from jax import lax
import numpy as np

D_MODEL = 1024
BATCH = 32
SEQ = 256
DEPTH = 4
DEC_BATCH = 2
DEC_SEQ = 2048
PAST_LEN = 512

GRID_W = 64
BLOCK = 128
EPS = 1e-6
ROPE_BASE = 10000.0
NEG = -1e30
MLA_HEADS = 8
MLA_NOPE = 64
MLA_ROPE = 32
MLA_V = 64
MLA_Q_RANK = 256
MLA_KV_RANK = 128
MLA_SCALE = (MLA_NOPE + MLA_ROPE) ** -0.5
SWA_HEADS = 4
SWA_KV_HEADS = 2
SWA_GROUP = SWA_HEADS // SWA_KV_HEADS
SWA_DIM = 64
SWA_WINDOW = 128
SWA_SCALE = SWA_DIM ** -0.5
CONV_CH = 256
CONV_WIDTH = 31
MIX_MLA = MLA_HEADS * MLA_V
MIX_SWA = SWA_HEADS * SWA_DIM
MIX_WIDTH = MIX_MLA + MIX_SWA + CONV_CH
IN_SPLITS = (MLA_Q_RANK, MLA_KV_RANK, MLA_ROPE, SWA_HEADS * SWA_DIM, SWA_KV_HEADS * SWA_DIM, SWA_KV_HEADS * SWA_DIM, 2 * CONV_CH)
IN_COLS = sum(IN_SPLITS)
D_FF = 2816
FFN_WIDTH = 3

kernel_name = "hybrid_mla_swa_conv_diffusion_step"


def rmsnorm(x, g):
    xf = x.astype(jnp.float32)
    y = xf * lax.rsqrt(jnp.mean(xf * xf, axis=-1, keepdims=True) + EPS)
    return (y * g.astype(jnp.float32)).astype(x.dtype)


def layernorm(x, g, b):
    xf = x.astype(jnp.float32)
    mu = jnp.mean(xf, axis=-1, keepdims=True)
    var = jnp.mean(jnp.square(xf - mu), axis=-1, keepdims=True)
    y = (xf - mu) * lax.rsqrt(var + EPS)
    return (y * g.astype(jnp.float32) + b.astype(jnp.float32)).astype(x.dtype)


def grid_positions(T):
    rows = T // GRID_W
    row = jnp.repeat(jnp.arange(rows), GRID_W).astype(jnp.float32)
    col = jnp.tile(jnp.arange(GRID_W), rows).astype(jnp.float32)
    return row, col


def rope_2d(x, row, col):
    d = x.shape[-1]
    qd = d // 4
    T = x.shape[1]
    extra = x.ndim - 3
    freqs = ROPE_BASE ** (-jnp.arange(qd, dtype=jnp.float32) / qd)

    def rot(xh, pos):
        ang = pos[:, None] * freqs[None, :]
        cos = jnp.cos(ang).reshape((1, T) + (1,) * extra + (qd,)).astype(x.dtype)
        sin = jnp.sin(ang).reshape((1, T) + (1,) * extra + (qd,)).astype(x.dtype)
        x1, x2 = xh[..., :qd], xh[..., qd:]
        return jnp.concatenate([x1 * cos - x2 * sin, x2 * cos + x1 * sin], axis=-1)

    return jnp.concatenate([rot(x[..., : d // 2], row), rot(x[..., d // 2:], col)], axis=-1)


def dwconv(x, w, b):
    K, C = w.shape
    pad = (K - 1) // 2
    y = lax.conv_general_dilated(x, w[:, None, :].astype(x.dtype), window_strides=(1,), padding=[(pad, pad)],
                                 dimension_numbers=("NWC", "WIO", "NWC"), feature_group_count=C)
    return y + b


def dense_attention(q, k, v, scale, sink=None):
    B, Tq, KH, G, dk = q.shape
    Tk = k.shape[1]
    nq = Tq // BLOCK
    qb = jnp.moveaxis(q.reshape(B, nq, BLOCK, KH, G, dk), 1, 0)

    def one_block(qi):
        s = jnp.einsum("bqhgd,bkhd->bhgqk", qi, k).astype(jnp.float32) * scale
        if sink is not None:
            sk = jnp.broadcast_to(sink.astype(jnp.float32)[None, :, :, None, None], s.shape[:-1] + (1,))
            s = jnp.concatenate([s, sk], axis=-1)
        p = jax.nn.softmax(s, axis=-1)[..., :Tk]
        return jnp.einsum("bhgqk,bkhd->bqhgd", p.astype(v.dtype), v)

    ob = lax.map(one_block, qb)
    return jnp.moveaxis(ob, 0, 1).reshape(B, Tq, KH, G, v.shape[-1])


def banded_window_attention(q, k, v, kc, vc, sink, scale):
    B, T, KH, G, d = q.shape
    L = kc.shape[1]
    nb = T // BLOCK
    nk = 3 * BLOCK
    qb = q.reshape(B, nb, BLOCK, KH, G, d)

    def band(a):
        ap = jnp.pad(a, ((0, 0), (BLOCK, BLOCK), (0, 0), (0, 0))).reshape(B, nb + 2, BLOCK, KH, d)
        return jnp.concatenate([ap[:, :-2], ap[:, 1:-1], ap[:, 2:]], axis=2)

    kb, vb = band(k), band(v)
    qpos = jnp.arange(nb)[:, None] * BLOCK + jnp.arange(BLOCK)[None, :]
    kpos = (jnp.arange(nb)[:, None] - 1) * BLOCK + jnp.arange(nk)[None, :]
    valid = (kpos[:, None, :] >= 0) & (kpos[:, None, :] < T) & (jnp.abs(qpos[:, :, None] - kpos[:, None, :]) <= SWA_WINDOW)
    s_band = jnp.einsum("bnqhgd,bnkhd->bnhgqk", qb, kb).astype(jnp.float32) * scale
    s_band = jnp.where(valid[None, :, None, None], s_band, NEG)
    s_ctx = jnp.einsum("bnqhgd,bkhd->bnhgqk", qb, kc).astype(jnp.float32) * scale
    s_sink = jnp.broadcast_to(sink.astype(jnp.float32)[None, None, :, :, None, None], s_band.shape[:-1] + (1,))
    p = jax.nn.softmax(jnp.concatenate([s_band, s_ctx, s_sink], axis=-1), axis=-1).astype(v.dtype)
    o = jnp.einsum("bnhgqk,bnkhd->bnqhgd", p[..., :nk], vb) + jnp.einsum("bnhgqk,bkhd->bnqhgd", p[..., nk:nk + L], vc)
    return o.reshape(B, T, KH, G, d)


def modulation(cvec, p):
    m = jax.nn.silu(cvec) @ p["ada_w"] + p["ada_b"]
    return [t[:, None, :] for t in jnp.split(m, 6, axis=-1)]


def modulate(x, g, shift, scale):
    return rmsnorm(x, g) * (1 + scale) + shift


def layer_mixer_inputs(h, p):
    B, T, _ = h.shape
    z = h @ p["w_in"]
    points, acc = [], 0
    for s in IN_SPLITS[:-1]:
        acc += s
        points.append(acc)
    cq, ckv, kr, qs, ks, vs, cu = jnp.split(z, points, axis=-1)
    q_m = (rmsnorm(cq, p["mla_q_norm_g"]) @ p["mla_w_uq"]).reshape(B, T, MLA_HEADS, MLA_NOPE + MLA_ROPE)
    ckv_n = rmsnorm(ckv, p["mla_kv_norm_g"])
    q_s = qs.reshape(B, T, SWA_KV_HEADS, SWA_GROUP, SWA_DIM)
    k_s = ks.reshape(B, T, SWA_KV_HEADS, SWA_DIM)
    v_s = vs.reshape(B, T, SWA_KV_HEADS, SWA_DIM)
    return q_m, ckv_n, kr, q_s, k_s, v_s, cu


def mla_expand(ckv_n, kr, w_ukv):
    B, L, _ = ckv_n.shape
    kv = (ckv_n @ w_ukv).reshape(B, L, MLA_HEADS, MLA_NOPE + MLA_V)
    k = jnp.concatenate([kv[..., :MLA_NOPE], jnp.broadcast_to(kr[:, :, None, :], (B, L, MLA_HEADS, MLA_ROPE))], axis=-1)
    return k, kv[..., MLA_NOPE:]


def conv_module(u, p):
    y = u[..., :CONV_CH] * jax.nn.sigmoid(u[..., CONV_CH:])
    y = dwconv(y, p["conv_dw_w"], p["conv_dw_b"])
    y = jax.nn.silu(layernorm(y, p["conv_ln_g"], p["conv_ln_b"]))
    return y @ p["conv_w_pw2"]


def mixer_output(o_mla, o_swa, cu, p):
    B, T = o_mla.shape[:2]
    o = jnp.concatenate([o_mla.reshape(B, T, MIX_MLA), o_swa.reshape(B, T, MIX_SWA), conv_module(cu, p)], axis=-1)
    return o @ p["w_out"]


def conv_ffn(h, p):
    z = dwconv(h @ p["ffn_w_up"], p["ffn_dw_w"], p["ffn_dw_b"])
    return (jax.nn.silu(z[..., D_FF:]) * z[..., :D_FF]) @ p["ffn_w_down"]


def context_layer(x, c_ctx, p):
    sh1, sc1, g1, sh2, sc2, g2 = modulation(c_ctx[None, :], p)
    h = modulate(x, p["norm1_g"], sh1, sc1)
    q_m, ckv_n, kr, q_s, k_s, v_s, cu = layer_mixer_inputs(h, p)
    k_m, v_m = mla_expand(ckv_n, kr, p["mla_w_ukv"])
    o_mla = dense_attention(q_m[:, :, :, None, :], k_m, v_m, MLA_SCALE)
    o_swa = dense_attention(q_s, k_s, v_s, SWA_SCALE, p["swa_sink"].reshape(SWA_KV_HEADS, SWA_GROUP))
    x = x + g1 * mixer_output(o_mla, o_swa, cu, p)
    x = x + g2 * conv_ffn(modulate(x, p["norm2_g"], sh2, sc2), p)
    return x, ckv_n, kr, k_s, v_s


def latent_layer(x, c, ckv_c, kr_c, k_c, v_c, p, row, col):
    sh1, sc1, g1, sh2, sc2, g2 = modulation(c, p)
    h = modulate(x, p["norm1_g"], sh1, sc1)
    q_m, ckv_n, kr, q_s, k_s, v_s, cu = layer_mixer_inputs(h, p)
    q_m = jnp.concatenate([q_m[..., :MLA_NOPE], rope_2d(q_m[..., MLA_NOPE:], row, col)], axis=-1)
    kr = rope_2d(kr, row, col)
    k_lat, v_lat = mla_expand(ckv_n, kr, p["mla_w_ukv"])
    k_ctx, v_ctx = mla_expand(ckv_c, kr_c, p["mla_w_ukv"])
    o_mla = dense_attention(q_m[:, :, :, None, :], jnp.concatenate([k_lat, k_ctx], axis=1),
                            jnp.concatenate([v_lat, v_ctx], axis=1), MLA_SCALE)
    o_swa = banded_window_attention(rope_2d(q_s, row, col), rope_2d(k_s, row, col), v_s, k_c, v_c,
                                    p["swa_sink"].reshape(SWA_KV_HEADS, SWA_GROUP), SWA_SCALE)
    x = x + g1 * mixer_output(o_mla, o_swa, cu, p)
    x = x + g2 * conv_ffn(modulate(x, p["norm2_g"], sh2, sc2), p)
    return x


def setup_inputs(seed: int = 0) -> dict:
    key = jax.random.key(seed)
    ks = jax.random.split(key, 32)
    f32 = jnp.float32
    nrm = lambda k, shape, s: jax.random.normal(k, shape, f32) * s
    gain = lambda k, shape: 1.0 + 0.05 * jax.random.normal(k, shape, f32)
    return {
        "x_prompt": nrm(ks[0], (BATCH, SEQ, D_MODEL), 1.0),
        "x_sample": nrm(ks[1], (DEC_BATCH, DEC_SEQ, D_MODEL), 1.0),
        "cache_mla_ckv": nrm(ks[2], (DEC_BATCH, DEPTH, PAST_LEN, MLA_KV_RANK), 1.0),
        "cache_mla_krope": nrm(ks[3], (DEC_BATCH, DEPTH, PAST_LEN, MLA_ROPE), 1.0),
        "cache_swa_k": nrm(ks[4], (DEC_BATCH, DEPTH, PAST_LEN, SWA_KV_HEADS, SWA_DIM), 1.0),
        "cache_swa_v": nrm(ks[5], (DEC_BATCH, DEPTH, PAST_LEN, SWA_KV_HEADS, SWA_DIM), 1.0),
        "c": nrm(ks[6], (DEC_BATCH, D_MODEL), 1.0),
        "c_ctx": nrm(ks[7], (D_MODEL,), 1.0),
        "ada_w": nrm(ks[8], (DEPTH, D_MODEL, 6 * D_MODEL), 0.5 * D_MODEL ** -0.5),
        "ada_b": nrm(ks[9], (DEPTH, 6 * D_MODEL), 0.01),
        "norm1_g": gain(ks[10], (DEPTH, D_MODEL)),
        "norm2_g": gain(ks[11], (DEPTH, D_MODEL)),
        "w_in": nrm(ks[12], (DEPTH, D_MODEL, IN_COLS), D_MODEL ** -0.5),
        "mla_q_norm_g": gain(ks[13], (DEPTH, MLA_Q_RANK)),
        "mla_w_uq": nrm(ks[14], (DEPTH, MLA_Q_RANK, MLA_HEADS * (MLA_NOPE + MLA_ROPE)), MLA_Q_RANK ** -0.5),
        "mla_kv_norm_g": gain(ks[15], (DEPTH, MLA_KV_RANK)),
        "mla_w_ukv": nrm(ks[16], (DEPTH, MLA_KV_RANK, MLA_HEADS * (MLA_NOPE + MLA_V)), MLA_KV_RANK ** -0.5),
        "swa_sink": nrm(ks[17], (DEPTH, SWA_HEADS), 0.5),
        "conv_dw_w": nrm(ks[18], (DEPTH, CONV_WIDTH, CONV_CH), CONV_WIDTH ** -0.5),
        "conv_dw_b": nrm(ks[19], (DEPTH, CONV_CH), 0.01),
        "conv_ln_g": gain(ks[20], (DEPTH, CONV_CH)),
        "conv_ln_b": nrm(ks[21], (DEPTH, CONV_CH), 0.01),
        "conv_w_pw2": nrm(ks[22], (DEPTH, CONV_CH, CONV_CH), CONV_CH ** -0.5),
        "w_out": nrm(ks[23], (DEPTH, MIX_WIDTH, D_MODEL), MIX_WIDTH ** -0.5),
        "ffn_w_up": nrm(ks[24], (DEPTH, D_MODEL, 2 * D_FF), D_MODEL ** -0.5),
        "ffn_dw_w": nrm(ks[25], (DEPTH, FFN_WIDTH, 2 * D_FF), FFN_WIDTH ** -0.5),
        "ffn_dw_b": nrm(ks[26], (DEPTH, 2 * D_FF), 0.01),
        "ffn_w_down": nrm(ks[27], (DEPTH, D_FF, D_MODEL), D_FF ** -0.5),
        "final_norm_g": gain(ks[28], (D_MODEL,)),
    }


def reference(x_prompt, x_sample, cache_mla_ckv, cache_mla_krope, cache_swa_k, cache_swa_v, c, c_ctx,
              ada_w, ada_b, norm1_g, norm2_g, w_in, mla_q_norm_g, mla_w_uq, mla_kv_norm_g, mla_w_ukv,
              swa_sink, conv_dw_w, conv_dw_b, conv_ln_g, conv_ln_b, conv_w_pw2, w_out,
              ffn_w_up, ffn_dw_w, ffn_dw_b, ffn_w_down, final_norm_g):
    row, col = grid_positions(x_sample.shape[1])
    xc, xl = x_prompt, x_sample
    new_ckv, new_kr, new_k, new_v = [], [], [], []
    for l in range(DEPTH):
        p = {
            "ada_w": ada_w[l], "ada_b": ada_b[l], "norm1_g": norm1_g[l], "norm2_g": norm2_g[l],
            "w_in": w_in[l], "mla_q_norm_g": mla_q_norm_g[l], "mla_w_uq": mla_w_uq[l],
            "mla_kv_norm_g": mla_kv_norm_g[l], "mla_w_ukv": mla_w_ukv[l], "swa_sink": swa_sink[l],
            "conv_dw_w": conv_dw_w[l], "conv_dw_b": conv_dw_b[l], "conv_ln_g": conv_ln_g[l],
            "conv_ln_b": conv_ln_b[l], "conv_w_pw2": conv_w_pw2[l], "w_out": w_out[l],
            "ffn_w_up": ffn_w_up[l], "ffn_dw_w": ffn_dw_w[l], "ffn_dw_b": ffn_dw_b[l],
            "ffn_w_down": ffn_w_down[l],
        }
        xc, ckv_n, kr, k_s, v_s = context_layer(xc, c_ctx, p)
        new_ckv.append(ckv_n)
        new_kr.append(kr)
        new_k.append(k_s)
        new_v.append(v_s)
        xl = latent_layer(xl, c, cache_mla_ckv[:, l], cache_mla_krope[:, l], cache_swa_k[:, l], cache_swa_v[:, l], p, row, col)
    y_prompt = rmsnorm(xc, final_norm_g)
    y_sample = rmsnorm(xl, final_norm_g)
    state_mla_ckv = jnp.stack(new_ckv, axis=1)
    state_mla_krope = jnp.stack(new_kr, axis=1)
    state_swa_k = jnp.stack(new_k, axis=1)
    state_swa_v = jnp.stack(new_v, axis=1)
    return (y_prompt, y_sample, state_mla_ckv, state_mla_krope, state_swa_k, state_swa_v)
```

```python
import functools

import jax
import jax.numpy as jnp
from jax import lax
from jax.experimental import pallas as pl
from jax.experimental.pallas import tpu as pltpu

D_MODEL = 1024
DEPTH = 4
GRID_W = 64
EPS = 1e-6
ROPE_BASE = 10000.0
NEG = -1e30
MLA_HEADS = 8
MLA_NOPE = 64
MLA_ROPE = 32
MLA_V = 64
MLA_Q_RANK = 256
MLA_KV_RANK = 128
MLA_SCALE = (MLA_NOPE + MLA_ROPE) ** -0.5
SWA_HEADS = 4
SWA_KV_HEADS = 2
SWA_GROUP = SWA_HEADS // SWA_KV_HEADS
SWA_DIM = 64
SWA_WINDOW = 128
SWA_SCALE = SWA_DIM ** -0.5
CONV_CH = 256
CONV_WIDTH = 31
CONV_PAD = (CONV_WIDTH - 1) // 2
D_FF = 2816
FFN_WIDTH = 3

LANES = 128
SUBLANES = 8
HEAD_SLOT = LANES
MLA_W = MLA_HEADS * HEAD_SLOT
IN_EXT = 1536
CONV_HALO = 16
FFN_HALO = SUBLANES
VMEM_LIMIT = 56 * 1024 * 1024

F32 = jnp.float32
BF16 = jnp.bfloat16


def _rms(x, g):
    return x * lax.rsqrt(jnp.mean(x * x, axis=-1, keepdims=True) + EPS) * g


def _dot(a, b):
    return jnp.dot(a, b, preferred_element_type=F32)


def _dot_t(a, b):
    return lax.dot_general(a, b, (((1,), (1,)), ((), ())), preferred_element_type=F32)


def _lane_iota(shape):
    return lax.broadcasted_iota(jnp.int32, shape, 1)


def _rope(x, cos, sin, qd):
    first = (_lane_iota(x.shape) % (2 * qd)) < qd
    swapped = jnp.where(first, pltpu.roll(x, LANES - qd, 1), pltpu.roll(x, qd, 1))
    return x * cos + swapped * sin


def _mod_kernel(c_ref, w_ref, b_ref, o_ref):
    c = c_ref[...]
    s = (c * jax.nn.sigmoid(c)).astype(BF16)
    o_ref[0] = _dot(s, w_ref[0].astype(BF16)) + b_ref[0]


def _modulation(cvec, ada_w, ada_b):
    nb = 2048
    return pl.pallas_call(
        _mod_kernel,
        out_shape=jax.ShapeDtypeStruct((DEPTH, SUBLANES, 6 * D_MODEL), F32),
        grid=(DEPTH, 6 * D_MODEL // nb),
        in_specs=[
            pl.BlockSpec((SUBLANES, D_MODEL), lambda l, j: (0, 0)),
            pl.BlockSpec((1, D_MODEL, nb), lambda l, j: (l, 0, j)),
            pl.BlockSpec((1, 1, nb), lambda l, j: (l, 0, j)),
        ],
        out_specs=pl.BlockSpec((1, SUBLANES, nb), lambda l, j: (l, 0, j)),
        compiler_params=pltpu.CompilerParams(dimension_semantics=("parallel", "parallel"), vmem_limit_bytes=VMEM_LIMIT),
        name="modulation",
    )(cvec, ada_w, ada_b.reshape(DEPTH, 1, 6 * D_MODEL))


def _cache_kernel(ckv_ref, kr_ref, wukv_ref, km_ref, kvm_ref):
    kv = _dot(ckv_ref[...].astype(BF16), wukv_ref[0])
    kr = kr_ref[...].astype(BF16)
    row = lax.broadcasted_iota(jnp.int32, (MLA_ROPE, LANES), 0)
    col = lax.broadcasted_iota(jnp.int32, (MLA_ROPE, LANES), 1)
    place = (col == row + MLA_NOPE).astype(BF16)
    krp = _dot(kr, place)
    keep = _lane_iota(krp.shape) < MLA_NOPE
    km = jnp.concatenate([jnp.where(keep, kv[:, h * LANES:(h + 1) * LANES], krp) for h in range(MLA_HEADS)], axis=1)
    km_ref[...] = km.astype(BF16)
    kvm_ref[...] = kv.astype(BF16)


def _expand_cache(cache_ckv, cache_kr, w_ukv):
    nb, _, L, _ = cache_ckv.shape
    out = jax.ShapeDtypeStruct((nb, DEPTH, L, MLA_W), BF16)
    return pl.pallas_call(
        _cache_kernel,
        out_shape=(out, out),
        grid=(nb, DEPTH),
        in_specs=[
            pl.BlockSpec((None, None, L, MLA_KV_RANK), lambda b, l: (b, l, 0, 0)),
            pl.BlockSpec((None, None, L, MLA_ROPE), lambda b, l: (b, l, 0, 0)),
            pl.BlockSpec((1, MLA_KV_RANK, MLA_W), lambda b, l: (l, 0, 0)),
        ],
        out_specs=(pl.BlockSpec((None, None, L, MLA_W), lambda b, l: (b, l, 0, 0)),
                   pl.BlockSpec((None, None, L, MLA_W), lambda b, l: (b, l, 0, 0))),
        compiler_params=pltpu.CompilerParams(dimension_semantics=("parallel", "parallel"), vmem_limit_bytes=VMEM_LIMIT),
        name="expand_cache",
    )(cache_ckv, cache_kr, w_ukv)


def _proj_kernel(*refs, rope):
    if rope:
        (x_ref, mod_ref, n1g_ref, win_ref, qg_ref, wuq_ref, kvg_ref, wukv_ref, cq_ref, sq_ref, cs_ref, ss_ref,
         qm_ref, km_ref, kvm_ref, ckv_ref, kr_ref, qs_ref, ks_ref, vs_ref, cy_ref) = refs
    else:
        (x_ref, mod_ref, n1g_ref, win_ref, qg_ref, wuq_ref, kvg_ref, wukv_ref,
         qm_ref, km_ref, kvm_ref, ckv_ref, kr_ref, qs_ref, ks_ref, vs_ref, cy_ref) = refs
    x = x_ref[...]
    h = _rms(x, n1g_ref[0]) * (1.0 + mod_ref[0, 1:2, :]) + mod_ref[0, 0:1, :]
    z = _dot(h.astype(BF16), win_ref[0])
    cq = _rms(z[:, 0:256], qg_ref[0])
    qm = _dot(cq.astype(BF16), wuq_ref[0]) * MLA_SCALE
    ckv = _rms(z[:, 256:384], kvg_ref[0])
    kv = _dot(ckv.astype(BF16), wukv_ref[0])
    krp = z[:, 384:512]
    qs = z[:, 512:768] * SWA_SCALE
    ks = z[:, 768:896]
    vs = z[:, 896:1024]
    cy = z[:, 1024:1280] * jax.nn.sigmoid(z[:, 1280:1536])
    ckv_ref[...] = ckv
    kr_ref[...] = krp[:, MLA_NOPE:MLA_NOPE + MLA_ROPE]
    if rope:
        cq_t, sq_t, cs_t, ss_t = cq_ref[...], sq_ref[...], cs_ref[...], ss_ref[...]
        qm = jnp.concatenate([_rope(qm[:, i * LANES:(i + 1) * LANES], cq_t, sq_t, MLA_ROPE // 4)
                              for i in range(MLA_HEADS)], axis=1)
        krp = _rope(krp, cq_t, sq_t, MLA_ROPE // 4)
        qs = jnp.concatenate([_rope(qs[:, i * LANES:(i + 1) * LANES], cs_t, ss_t, SWA_DIM // 4)
                              for i in range(SWA_GROUP)], axis=1)
        ks = _rope(ks, cs_t, ss_t, SWA_DIM // 4)
    keep = _lane_iota(krp.shape) < MLA_NOPE
    km = jnp.concatenate([jnp.where(keep, kv[:, i * LANES:(i + 1) * LANES], krp) for i in range(MLA_HEADS)], axis=1)
    qm_ref[...] = qm.astype(BF16)
    km_ref[...] = km.astype(BF16)
    kvm_ref[...] = kv.astype(BF16)
    qs_ref[...] = qs.astype(BF16)
    ks_ref[...] = ks.astype(ks_ref.dtype)
    vs_ref[...] = vs.astype(vs_ref.dtype)
    cy_ref[...] = cy


def _proj(x, mod, wl, l, *, seq, tm, rope_tabs, kv_dtype):
    n = x.shape[0]
    if mod.shape[0] == 1:
        bidx = lambda i: 0 * i
    else:
        per = seq // tm
        bidx = lambda i: i // per
    rope = rope_tabs is not None
    row = lambda w: pl.BlockSpec((tm, w), lambda i: (i, 0))
    full2 = lambda a: pl.BlockSpec(a.shape, lambda i: (0, 0))
    lay = lambda a: pl.BlockSpec((1,) + a.shape[1:], lambda i: (l, 0, 0))
    in_specs = [row(D_MODEL), pl.BlockSpec((1, 6, D_MODEL), lambda i: (bidx(i), 0, 0)),
                lay(wl["norm1_g"]), lay(wl["w_in"]), lay(wl["q_g"]), lay(wl["w_uq"]), lay(wl["kv_g"]), lay(wl["w_ukv"])]
    args = [x, mod, wl["norm1_g"], wl["w_in"], wl["q_g"], wl["w_uq"], wl["kv_g"], wl["w_ukv"]]
    if rope:
        tpb = seq // tm
        in_specs += [pl.BlockSpec((tm, LANES), lambda i: (i % tpb, 0))] * 4
        args += list(rope_tabs)
    out_shape = (
        jax.ShapeDtypeStruct((n, MLA_W), BF16), jax.ShapeDtypeStruct((n, MLA_W), BF16), jax.ShapeDtypeStruct((n, MLA_W), BF16),
        jax.ShapeDtypeStruct((n, MLA_KV_RANK), F32), jax.ShapeDtypeStruct((n, MLA_ROPE), F32),
        jax.ShapeDtypeStruct((n, SWA_GROUP * LANES), BF16),
        jax.ShapeDtypeStruct((n, LANES), kv_dtype), jax.ShapeDtypeStruct((n, LANES), kv_dtype),
        jax.ShapeDtypeStruct((n, CONV_CH), F32),
    )
    out_specs = (row(MLA_W), row(MLA_W), row(MLA_W), row(MLA_KV_RANK), row(MLA_ROPE), row(SWA_GROUP * LANES),
                 row(LANES), row(LANES), row(CONV_CH))
    return pl.pallas_call(
        functools.partial(_proj_kernel, rope=rope),
        out_shape=out_shape,
        grid=(n // tm,),
        in_specs=in_specs,
        out_specs=out_specs,
        compiler_params=pltpu.CompilerParams(dimension_semantics=("parallel",), vmem_limit_bytes=VMEM_LIMIT),
        name="proj_lat" if rope else "proj_ctx",
    )(*args)


def _attend(q, segs, sink=None):
    scores = []
    for k, _, mask in segs:
        s = _dot_t(q, k)
        if mask is not None:
            s = jnp.where(mask, s, NEG)
        scores.append(s)
    m = jnp.max(scores[0], axis=-1, keepdims=True)
    for s in scores[1:]:
        m = jnp.maximum(m, jnp.max(s, axis=-1, keepdims=True))
    if sink is not None:
        m = jnp.maximum(m, sink)
    den = jnp.zeros_like(m)
    out = None
    for s, (_, v, _) in zip(scores, segs):
        p = jnp.exp(s - m)
        den = den + jnp.sum(p, axis=-1, keepdims=True)
        pv = _dot(p.astype(BF16), v)
        out = pv if out is None else out + pv
    if sink is not None:
        den = den + jnp.exp(sink - m)
    return out / den


def _mix_kernel(*refs, latent, seq, tq):
    if latent:
        (sink_ref, x_ref, mod_ref, qm_ref, km_ref, kvm_ref, kmc_ref, kvmc_ref, qs_ref, ks_ref, vs_ref, ksc_ref, vsc_ref,
         cy_ref, dww_ref, dwb_ref, lng_ref, lnb_ref, pw2_ref, wout_ref, o_ref, ybuf) = refs
    else:
        (sink_ref, x_ref, mod_ref, qm_ref, km_ref, kvm_ref, qs_ref, ks_ref, vs_ref,
         cy_ref, dww_ref, dwb_ref, lng_ref, lnb_ref, pw2_ref, wout_ref, o_ref, ybuf) = refs
    j = pl.program_id(1)
    nq = seq // tq
    t0 = pl.multiple_of(j * tq, tq)
    lane = _lane_iota((tq, LANES))
    low = lane < (LANES // 2)

    mla = []
    for i in range(MLA_HEADS):
        sl = slice(i * LANES, (i + 1) * LANES)
        segs = [(km_ref[:, sl], kvm_ref[:, sl], None)]
        if latent:
            segs.append((kmc_ref[:, sl], kvmc_ref[:, sl], None))
        mla.append(_attend(qm_ref[:, sl], segs))
    o_mla = jnp.concatenate(
        [jnp.where(low, pltpu.roll(mla[2 * i], LANES // 2, 1), mla[2 * i + 1]) for i in range(MLA_HEADS // 2)], axis=1)

    if latent:
        win = 2 * SWA_WINDOW + tq
        ws = pl.multiple_of(jnp.clip(t0 - SWA_WINDOW, 0, seq - win), SWA_WINDOW)
        ks_w = ks_ref[pl.ds(ws, win), :]
        vs_w = vs_ref[pl.ds(ws, win), :]
        qpos = t0 + lax.broadcasted_iota(jnp.int32, (tq, win), 0)
        kpos = ws + lax.broadcasted_iota(jnp.int32, (tq, win), 1)
        band = jnp.abs(qpos - kpos) <= SWA_WINDOW
        ksc = ksc_ref[...].astype(BF16)
        vsc = vsc_ref[...].astype(BF16)
        swa_segs = [(ks_w, vs_w, band), (ksc, vsc, None)]
    else:
        swa_segs = [(ks_ref[...].astype(BF16), vs_ref[...].astype(BF16), None)]
    swa = []
    for g in range(SWA_GROUP):
        qg = qs_ref[:, g * LANES:(g + 1) * LANES]
        halves = []
        for kh in range(SWA_KV_HEADS):
            qh = jnp.where(low if kh == 0 else jnp.logical_not(low), qg, jnp.zeros_like(qg))
            halves.append(_attend(qh, swa_segs, sink_ref[kh * SWA_GROUP + g]))
        swa.append(jnp.where(low, halves[0], halves[1]))
    o_swa = jnp.concatenate(swa, axis=1)

    ybuf[CONV_HALO:CONV_HALO + tq, :] = cy_ref[pl.ds(t0, tq), :]
    if latent:
        prev = cy_ref[pl.ds(pl.multiple_of(jnp.maximum(t0 - CONV_HALO, 0), CONV_HALO), CONV_HALO), :]
        nxt = cy_ref[pl.ds(pl.multiple_of(jnp.minimum(t0 + tq, seq - CONV_HALO), CONV_HALO), CONV_HALO), :]
        ybuf[0:CONV_HALO, :] = jnp.where(j > 0, prev, 0.0)
        ybuf[CONV_HALO + tq:, :] = jnp.where(j < nq - 1, nxt, 0.0)
    else:
        ybuf[0:CONV_HALO, :] = jnp.zeros((CONV_HALO, CONV_CH), F32)
        ybuf[CONV_HALO + tq:, :] = jnp.zeros((CONV_HALO, CONV_CH), F32)
    acc = jnp.zeros((tq, CONV_CH), F32) + dwb_ref[0]
    for k in range(CONV_WIDTH):
        off = CONV_HALO - CONV_PAD + k
        acc = acc + dww_ref[0, k:k + 1, :] * ybuf[off:off + tq, :]
    mu = jnp.mean(acc, axis=-1, keepdims=True)
    var = jnp.mean(jnp.square(acc - mu), axis=-1, keepdims=True)
    yn = (acc - mu) * lax.rsqrt(var + EPS) * lng_ref[0] + lnb_ref[0]
    yn = yn * jax.nn.sigmoid(yn)
    o_conv = _dot(yn.astype(BF16), pw2_ref[0])

    o = jnp.concatenate([o_mla.astype(BF16), o_swa.astype(BF16), o_conv.astype(BF16)], axis=1)
    o_ref[...] = x_ref[...] + mod_ref[0, 2:3, :] * _dot(o, wout_ref[0])


def _mix(x, mod, sink, pj, wl, l, *, nb, seq, tq, cache):
    qm, km, kvm, _, _, qs, ks, vs, cy = pj
    latent = cache is not None
    nq = seq // tq
    mod_b = (lambda b: b) if mod.shape[0] > 1 else (lambda b: 0 * b)
    qrow = lambda w: pl.BlockSpec((tq, w), lambda b, j: (b * nq + j, 0))
    seqblk = lambda w: pl.BlockSpec((seq, w), lambda b, j: (b, 0))
    lay = lambda a: pl.BlockSpec((1,) + a.shape[1:], lambda b, j: (l, 0, 0))
    in_specs = [pl.BlockSpec(memory_space=pltpu.SMEM), qrow(D_MODEL),
                pl.BlockSpec((1, 6, D_MODEL), lambda b, j: (mod_b(b), 0, 0)),
                qrow(MLA_W), seqblk(MLA_W), seqblk(MLA_W)]
    args = [sink, x, mod, qm, km, kvm]
    if latent:
        kmc, kvmc, ksc, vsc = cache
        L = kmc.shape[2]
        cspec = lambda w: pl.BlockSpec((None, None, L, w), lambda b, j: (b, l, 0, 0))
        in_specs += [cspec(MLA_W), cspec(MLA_W)]
        args += [kmc, kvmc]
    in_specs += [qrow(SWA_GROUP * LANES), seqblk(LANES), seqblk(LANES)]
    args += [qs, ks, vs]
    if latent:
        in_specs += [cspec(LANES), cspec(LANES)]
        args += [ksc, vsc]
    in_specs += [seqblk(CONV_CH), lay(wl["dw_w"]), lay(wl["dw_b"]), lay(wl["ln_g"]), lay(wl["ln_b"]), lay(wl["pw2"]), lay(wl["w_out"])]
    args += [cy, wl["dw_w"], wl["dw_b"], wl["ln_g"], wl["ln_b"], wl["pw2"], wl["w_out"]]
    return pl.pallas_call(
        functools.partial(_mix_kernel, latent=latent, seq=seq, tq=tq),
        out_shape=jax.ShapeDtypeStruct(x.shape, F32),
        grid=(nb, nq),
        in_specs=in_specs,
        out_specs=qrow(D_MODEL),
        scratch_shapes=[pltpu.VMEM((tq + 2 * CONV_HALO, CONV_CH), F32)],
        compiler_params=pltpu.CompilerParams(dimension_semantics=("parallel", "parallel"), vmem_limit_bytes=VMEM_LIMIT),
        name="mix_lat" if latent else "mix_ctx",
    )(*args)


def _ffn_kernel(xp_ref, x_ref, xn_ref, mod_ref, n2g_ref, wup_ref, dww_ref, dwb_ref, wdn_ref, fg_ref, o_ref, *, seq, tm, chunk, final):
    i = pl.program_id(0)
    x = x_ref[...]
    xh = jnp.concatenate([xp_ref[...], x, xn_ref[...]], axis=0)
    h = (_rms(xh, n2g_ref[0]) * (1.0 + mod_ref[0, 4:5, :]) + mod_ref[0, 3:4, :]).astype(BF16)
    tpos = (i * tm + lax.broadcasted_iota(jnp.int32, (tm, 1), 0)) % seq
    has_prev = tpos > 0
    has_next = tpos < seq - 1
    rows = tm + 2 * FFN_HALO

    def conv3(zc, c0):
        w = dww_ref[0, :, c0:c0 + chunk]
        zm = jnp.where(has_prev, pltpu.roll(zc, 1, 0)[FFN_HALO:FFN_HALO + tm], 0.0)
        zp = jnp.where(has_next, pltpu.roll(zc, rows - 1, 0)[FFN_HALO:FFN_HALO + tm], 0.0)
        return zm * w[0:1] + zc[FFN_HALO:FFN_HALO + tm] * w[1:2] + zp * w[2:3] + dwb_ref[0, :, c0:c0 + chunk]

    acc = jnp.zeros((tm, D_MODEL), F32)
    for c in range(D_FF // chunk):
        c0 = c * chunk
        za = conv3(_dot(h, wup_ref[0, :, c0:c0 + chunk]), c0)
        zg = conv3(_dot(h, wup_ref[0, :, D_FF + c0:D_FF + c0 + chunk]), D_FF + c0)
        act = (zg * jax.nn.sigmoid(zg)) * za
        acc = acc + _dot(act.astype(BF16), wdn_ref[0, c0:c0 + chunk, :])
    y = x + mod_ref[0, 5:6, :] * acc
    if final:
        y = _rms(y, fg_ref[...])
    o_ref[...] = y


def _ffn(x, mod, wl, l, final_g, *, seq, tm, chunk, final):
    n = x.shape[0]
    if mod.shape[0] == 1:
        bidx = lambda i: 0 * i
    else:
        per = seq // tm
        bidx = lambda i: i // per
    hb = tm // FFN_HALO
    nh = n // FFN_HALO
    lay = lambda a: pl.BlockSpec((1,) + a.shape[1:], lambda i: (l, 0, 0))
    in_specs = [
        pl.BlockSpec((FFN_HALO, D_MODEL), lambda i: (jnp.maximum(i * hb - 1, 0), 0)),
        pl.BlockSpec((tm, D_MODEL), lambda i: (i, 0)),
        pl.BlockSpec((FFN_HALO, D_MODEL), lambda i: (jnp.minimum((i + 1) * hb, nh - 1), 0)),
        pl.BlockSpec((1, 6, D_MODEL), lambda i: (bidx(i), 0, 0)),
        lay(wl["norm2_g"]), lay(wl["w_up"]), lay(wl["ffn_dw_w"]), lay(wl["ffn_dw_b"]), lay(wl["w_down"]),
        pl.BlockSpec((1, D_MODEL), lambda i: (0, 0)),
    ]
    return pl.pallas_call(
        functools.partial(_ffn_kernel, seq=seq, tm=tm, chunk=chunk, final=final),
        out_shape=jax.ShapeDtypeStruct(x.shape, F32),
        grid=(n // tm,),
        in_specs=in_specs,
        out_specs=pl.BlockSpec((tm, D_MODEL), lambda i: (i, 0)),
        compiler_params=pltpu.CompilerParams(dimension_semantics=("parallel",), vmem_limit_bytes=VMEM_LIMIT),
        name="ffn",
    )(x, x, x, mod, wl["norm2_g"], wl["w_up"], wl["ffn_dw_w"], wl["ffn_dw_b"], wl["w_down"], final_g)


def _prep_weights(norm1_g, norm2_g, w_in, mla_q_norm_g, mla_w_uq, mla_kv_norm_g, mla_w_ukv, conv_dw_w, conv_dw_b,
                  conv_ln_g, conv_ln_b, conv_w_pw2, w_out, ffn_w_up, ffn_dw_w, ffn_dw_b, ffn_w_down):
    z = lambda n: jnp.zeros((DEPTH, D_MODEL, n), w_in.dtype)
    qs0 = 416
    head = lambda hd: w_in[:, :, qs0 + hd * SWA_DIM: qs0 + (hd + 1) * SWA_DIM]
    w_in_ext = jnp.concatenate([
        w_in[:, :, 0:384], z(MLA_NOPE), w_in[:, :, 384:416], z(LANES - MLA_NOPE - MLA_ROPE),
        head(0), head(2), head(1), head(3),
        w_in[:, :, 672:1440]], axis=2).astype(BF16)
    assert w_in_ext.shape[2] == IN_EXT
    w_uq = mla_w_uq.reshape(DEPTH, MLA_Q_RANK, MLA_HEADS, MLA_NOPE + MLA_ROPE)
    w_uq = jnp.pad(w_uq, ((0, 0), (0, 0), (0, 0), (0, HEAD_SLOT - MLA_NOPE - MLA_ROPE))).reshape(DEPTH, MLA_Q_RANK, MLA_W)
    so = MLA_HEADS * MLA_V
    orow = lambda hd: w_out[:, so + hd * SWA_DIM: so + (hd + 1) * SWA_DIM, :]
    w_out_p = jnp.concatenate([w_out[:, :so], orow(0), orow(2), orow(1), orow(3), w_out[:, so + SWA_HEADS * SWA_DIM:]], axis=1)
    r3 = lambda a: a.reshape(DEPTH, 1, a.shape[-1])
    return {
        "norm1_g": r3(norm1_g), "norm2_g": r3(norm2_g), "w_in": w_in_ext,
        "q_g": r3(mla_q_norm_g), "w_uq": w_uq.astype(BF16), "kv_g": r3(mla_kv_norm_g), "w_ukv": mla_w_ukv.astype(BF16),
        "dw_w": conv_dw_w, "dw_b": r3(conv_dw_b), "ln_g": r3(conv_ln_g), "ln_b": r3(conv_ln_b),
        "pw2": conv_w_pw2.astype(BF16), "w_out": w_out_p.astype(BF16),
        "w_up": ffn_w_up.astype(BF16), "ffn_dw_w": ffn_dw_w, "ffn_dw_b": r3(ffn_dw_b), "w_down": ffn_w_down.astype(BF16),
    }


def _rope_tables(T):
    rows = T // GRID_W
    row = jnp.repeat(jnp.arange(rows), GRID_W).astype(F32)
    col = jnp.tile(jnp.arange(GRID_W), rows).astype(F32)

    def tabs(d):
        qd = d // 4
        freqs = ROPE_BASE ** (-jnp.arange(qd, dtype=F32) / qd)
        ar = row[:, None] * freqs[None, :]
        ac = col[:, None] * freqs[None, :]
        cos = jnp.concatenate([jnp.cos(ar), jnp.cos(ar), jnp.cos(ac), jnp.cos(ac)], axis=1)
        sin = jnp.concatenate([-jnp.sin(ar), jnp.sin(ar), -jnp.sin(ac), jnp.sin(ac)], axis=1)
        return cos, sin

    cm, sm = tabs(MLA_ROPE)
    padl, padr = MLA_NOPE, LANES - MLA_NOPE - MLA_ROPE
    cq = jnp.concatenate([jnp.ones((T, padl), F32), cm, jnp.ones((T, padr), F32)], axis=1)
    sq = jnp.concatenate([jnp.zeros((T, padl), F32), sm, jnp.zeros((T, padr), F32)], axis=1)
    c64, s64 = tabs(SWA_DIM)
    cs = jnp.concatenate([c64, c64], axis=1)
    ss = jnp.concatenate([s64, s64], axis=1)
    return cq, sq, cs, ss


def kernel(x_prompt, x_sample, cache_mla_ckv, cache_mla_krope, cache_swa_k, cache_swa_v, c, c_ctx, ada_w, ada_b, norm1_g, norm2_g, w_in, mla_q_norm_g, mla_w_uq, mla_kv_norm_g, mla_w_ukv, swa_sink, conv_dw_w, conv_dw_b, conv_ln_g, conv_ln_b, conv_w_pw2, w_out, ffn_w_up, ffn_dw_w, ffn_dw_b, ffn_w_down, final_norm_g):
    B, S, _ = x_prompt.shape
    NB, T, _ = x_sample.shape
    L = cache_mla_ckv.shape[2]
    wl = _prep_weights(norm1_g, norm2_g, w_in, mla_q_norm_g, mla_w_uq, mla_kv_norm_g, mla_w_ukv, conv_dw_w, conv_dw_b,
                       conv_ln_g, conv_ln_b, conv_w_pw2, w_out, ffn_w_up, ffn_dw_w, ffn_dw_b, ffn_w_down)
    cvec = jnp.concatenate([c_ctx[None, :], c, jnp.zeros((SUBLANES - 1 - NB, D_MODEL), F32)], axis=0)
    mod = _modulation(cvec, ada_w, ada_b).reshape(DEPTH, SUBLANES, 6, D_MODEL)
    kmc, kvmc = _expand_cache(cache_mla_ckv, cache_mla_krope, wl["w_ukv"])
    ksc = cache_swa_k.reshape(NB, DEPTH, L, SWA_KV_HEADS * SWA_DIM)
    vsc = cache_swa_v.reshape(NB, DEPTH, L, SWA_KV_HEADS * SWA_DIM)
    tabs = _rope_tables(T)
    fg = final_norm_g.reshape(1, D_MODEL)

    xc = x_prompt.reshape(B * S, D_MODEL)
    xl = x_sample.reshape(NB * T, D_MODEL)
    st_ckv, st_kr, st_k, st_v = [], [], [], []
    for l in range(DEPTH):
        last = l == DEPTH - 1
        mod_c = mod[l, 0:1]
        mod_l = mod[l, 1:1 + NB]
        pj = _proj(xc, mod_c, wl, l, seq=S, tm=512, rope_tabs=None, kv_dtype=F32)
        st_ckv.append(pj[3]); st_kr.append(pj[4]); st_k.append(pj[6]); st_v.append(pj[7])
        xc = _mix(xc, mod_c, swa_sink[l], pj, wl, l, nb=B, seq=S, tq=S, cache=None)
        xc = _ffn(xc, mod_c, wl, l, fg, seq=S, tm=512, chunk=256, final=last)

        pj = _proj(xl, mod_l, wl, l, seq=T, tm=512, rope_tabs=tabs, kv_dtype=BF16)
        xl = _mix(xl, mod_l, swa_sink[l], pj, wl, l, nb=NB, seq=T, tq=256, cache=(kmc, kvmc, ksc, vsc))
        xl = _ffn(xl, mod_l, wl, l, fg, seq=T, tm=512, chunk=256, final=last)

    stack = lambda xs, shp: jnp.stack([a.reshape((B, S) + shp) for a in xs], axis=1)
    return (xc.reshape(B, S, D_MODEL), xl.reshape(NB, T, D_MODEL),
            stack(st_ckv, (MLA_KV_RANK,)), stack(st_kr, (MLA_ROPE,)),
            stack(st_k, (SWA_KV_HEADS, SWA_DIM)), stack(st_v, (SWA_KV_HEADS, SWA_DIM)))
```
